```python
import jax, jax.numpy as jnp
from jax import lax
import numpy as np

D_MODEL = 2048
BATCH = 16
SEQ = 2048
DEPTH = 2

HEAD_DIM = 128
POOL_WINDOWS = (2, 4, 8, 16)
N_POOL_GROUPS = len(POOL_WINDOWS)
POOL_WIDTH = D_MODEL // 4
POOL_GROUP = POOL_WIDTH // N_POOL_GROUPS
SCONV_WIDTH = 3 * D_MODEL // 8
CONF_WIDTH = D_MODEL - POOL_WIDTH - SCONV_WIDTH
N_CONF_HEADS = CONF_WIDTH // HEAD_DIM
SCONV_K = 3
CONF_K = 31
IN_PROJ_WIDTH = POOL_WIDTH + 3 * SCONV_WIDTH + 2 * CONF_WIDTH
MIX_WIDTH = POOL_WIDTH + SCONV_WIDTH + CONF_WIDTH

PEER_HEADS = 8
PEER_NKEYS = 128
PEER_EXPERTS = PEER_NKEYS * PEER_NKEYS
PEER_TOPK = 16
PEER_QHALF = 128
PEER_QDIM = 2 * PEER_QHALF
PEER_BLOCK = 128
EPS = 1e-6

kernel_name = "hybrid_pool_shortconv_conformer_peer"


def rmsnorm(x, g):
    x32 = x.astype(jnp.float32)
    y = x32 * lax.rsqrt(jnp.mean(x32 * x32, axis=-1, keepdims=True) + EPS)
    return (y * g.astype(jnp.float32)).astype(x.dtype)


def causal_depthwise_conv(x, w):
    k_width = w.shape[0]
    s = x.shape[1]
    xp = jnp.pad(x, ((0, 0), (k_width - 1, 0), (0, 0)))
    y = xp[:, 0:s] * w[0]
    for k in range(1, k_width):
        y = y + xp[:, k:k + s] * w[k]
    return y


def head_layernorm(x, g, b, n_heads):
    bsz, s, c = x.shape
    x32 = x.astype(jnp.float32).reshape(bsz, s, n_heads, c // n_heads)
    mu = jnp.mean(x32, axis=-1, keepdims=True)
    xc = x32 - mu
    var = jnp.mean(xc * xc, axis=-1, keepdims=True)
    y = (xc * lax.rsqrt(var + EPS)).reshape(bsz, s, c)
    return (y * g.astype(jnp.float32) + b.astype(jnp.float32)).astype(x.dtype)


def pool_mixer(xa, pool_w, pool_scale):
    bsz, s, _ = xa.shape
    x32 = xa.astype(jnp.float32)
    c = jnp.cumsum(x32, axis=1)
    count = jnp.arange(1, s + 1, dtype=jnp.float32)[None, :, None]
    groups = []
    for gi, win in enumerate(POOL_WINDOWS):
        cg = c[..., gi * POOL_GROUP:(gi + 1) * POOL_GROUP]
        prev = jnp.pad(cg, ((0, 0), (win, 0), (0, 0)))[:, :s]
        groups.append((cg - prev) / jnp.minimum(count, float(win)))
    pooled = (jnp.concatenate(groups, axis=-1) - x32).astype(xa.dtype)
    pooled = pooled.reshape(bsz, s, N_POOL_GROUPS, POOL_GROUP)
    y = jnp.einsum('bsgc,gcd->bsgd', pooled, pool_w).reshape(bsz, s, POOL_WIDTH)
    return y * pool_scale


def hybrid_mixer(xn, w_in, pool_w, pool_scale, sconv_w, conf_w, conf_b, conf_ln_g, conf_ln_b, w_out):
    z = xn @ w_in
    o1 = POOL_WIDTH
    o2 = o1 + SCONV_WIDTH
    o3 = o2 + SCONV_WIDTH
    o4 = o3 + SCONV_WIDTH
    o5 = o4 + CONF_WIDTH
    xa = z[..., :o1]
    h_b, gate_b, gate_c = z[..., o1:o2], z[..., o2:o3], z[..., o3:o4]
    conf_val, conf_gate = z[..., o4:o5], z[..., o5:]

    y_a = pool_mixer(xa, pool_w, pool_scale)
    y_b = gate_b * causal_depthwise_conv(gate_c * h_b, sconv_w)
    u = conf_val * jax.nn.sigmoid(conf_gate)
    u = causal_depthwise_conv(u, conf_w) + conf_b
    u = head_layernorm(u, conf_ln_g, conf_ln_b, N_CONF_HEADS)
    y_c = u * jax.nn.sigmoid(u)

    y = jnp.concatenate([y_a, y_b, y_c], axis=-1)
    return y @ w_out


def peer_ffn(xn, wq, k1, k2, u, v):
    bsz, s, d = xn.shape
    t = bsz * s
    xt = xn.reshape(t, d)
    q = (xt @ wq).astype(jnp.float32).reshape(t, PEER_HEADS, 2, PEER_QHALF)
    s1 = jnp.einsum('thc,hnc->thn', q[:, :, 0], k1.astype(jnp.float32))
    s2 = jnp.einsum('thc,hnc->thn', q[:, :, 1], k2.astype(jnp.float32))
    t1, i1 = lax.top_k(s1, PEER_TOPK)
    t2, i2 = lax.top_k(s2, PEER_TOPK)
    cand = (t1[..., :, None] + t2[..., None, :]).reshape(t, PEER_HEADS, PEER_TOPK * PEER_TOPK)
    cidx = (i1[..., :, None] * PEER_NKEYS + i2[..., None, :]).reshape(t, PEER_HEADS, PEER_TOPK * PEER_TOPK)
    top, pos = lax.top_k(cand, PEER_TOPK)
    eidx = jnp.take_along_axis(cidx, pos, axis=-1).reshape(t, PEER_HEADS * PEER_TOPK)
    gate = jax.nn.softmax(top, axis=-1).reshape(t, PEER_HEADS * PEER_TOPK).astype(xn.dtype)

    n_blocks = t // PEER_BLOCK

    def expert_block(args):
        xc, ic, gc = args
        ug = u[ic]
        h = jax.nn.gelu(jnp.einsum('ckd,cd->ck', ug, xc), approximate=False)
        return jnp.einsum('ck,ckd->cd', gc * h, v[ic])

    out = lax.map(expert_block, (xt.reshape(n_blocks, PEER_BLOCK, d),
                                 eidx.reshape(n_blocks, PEER_BLOCK, -1),
                                 gate.reshape(n_blocks, PEER_BLOCK, -1)))
    return out.reshape(bsz, s, d)


def setup_inputs(seed: int = 0) -> dict:
    key = jax.random.key(seed)
    ks = jax.random.split(key, 20)
    f32 = jnp.float32
    L, D = DEPTH, D_MODEL

    def nrm(k, shape, scale):
        return jax.random.normal(k, shape, f32) * scale

    return {
        "x": jax.random.normal(ks[0], (BATCH, SEQ, D), f32),
        "norm1_g": 1.0 + nrm(ks[1], (L, D), 0.05),
        "w_in": nrm(ks[2], (L, D, IN_PROJ_WIDTH), D ** -0.5),
        "pool_w": nrm(ks[3], (L, N_POOL_GROUPS, POOL_GROUP, POOL_GROUP), POOL_GROUP ** -0.5),
        "pool_scale": 1.0 + nrm(ks[4], (L, POOL_WIDTH), 0.1),
        "sconv_w": nrm(ks[5], (L, SCONV_K, SCONV_WIDTH), SCONV_K ** -0.5),
        "conf_w": nrm(ks[6], (L, CONF_K, CONF_WIDTH), CONF_K ** -0.5),
        "conf_b": nrm(ks[7], (L, CONF_WIDTH), 0.02),
        "conf_ln_g": 1.0 + nrm(ks[8], (L, CONF_WIDTH), 0.05),
        "conf_ln_b": nrm(ks[9], (L, CONF_WIDTH), 0.02),
        "w_out": nrm(ks[10], (L, MIX_WIDTH, D), MIX_WIDTH ** -0.5),
        "norm2_g": 1.0 + nrm(ks[11], (L, D), 0.05),
        "peer_wq": nrm(ks[12], (L, D, PEER_HEADS * PEER_QDIM), D ** -0.5),
        "peer_k1": nrm(ks[13], (L, PEER_HEADS, PEER_NKEYS, PEER_QHALF), PEER_QHALF ** -0.5),
        "peer_k2": nrm(ks[14], (L, PEER_HEADS, PEER_NKEYS, PEER_QHALF), PEER_QHALF ** -0.5),
        "peer_u": nrm(ks[15], (L, PEER_EXPERTS, D), D ** -0.5),
        "peer_v": nrm(ks[16], (L, PEER_EXPERTS, D), PEER_HEADS ** -0.5),
        "final_g": 1.0 + nrm(ks[17], (D,), 0.05),
    }


def reference(x, norm1_g, w_in, pool_w, pool_scale, sconv_w, conf_w, conf_b, conf_ln_g, conf_ln_b,
              w_out, norm2_g, peer_wq, peer_k1, peer_k2, peer_u, peer_v, final_g):
    h = x
    for l in range(DEPTH):
        xn = rmsnorm(h, norm1_g[l])
        h = h + hybrid_mixer(xn, w_in[l], pool_w[l], pool_scale[l], sconv_w[l], conf_w[l], conf_b[l],
                             conf_ln_g[l], conf_ln_b[l], w_out[l])
        xn = rmsnorm(h, norm2_g[l])
        h = h + peer_ffn(xn, peer_wq[l], peer_k1[l], peer_k2[l], peer_u[l], peer_v[l])
    return rmsnorm(h, final_g)
```

```python
import functools

import jax
import jax.numpy as jnp
from jax import lax
from jax.experimental import pallas as pl
from jax.experimental.pallas import tpu as pltpu

F32 = jnp.float32
BF16 = jnp.bfloat16

EPS = 1e-6
LANES = 128
SUBLANES = 8
POOL_WINDOWS = (2, 4, 8, 16)
SCONV_K = 3
CONF_K = 31
PEER_HEADS = 8
PEER_NKEYS = 128
PEER_TOPK = 16
HIST = 32
ROW_CHUNK = 256
VMEM_LIMIT = 56 * 1024 * 1024


def _params(*sem):
    return pltpu.CompilerParams(dimension_semantics=sem, vmem_limit_bytes=VMEM_LIMIT)


def _tile(n, want):
    t = min(n, want)
    assert n % t == 0, (n, want)
    return t


def _rms_inproj_kernel(x_ref, g_ref, w_ref, z_ref, xn_ref):
    @pl.when(pl.program_id(1) == 0)
    def _():
        x = x_ref[...]
        ms = jnp.mean(x * x, axis=-1, keepdims=True)
        xn_ref[...] = (x * lax.rsqrt(ms + EPS) * g_ref[...]).astype(BF16)

    z_ref[...] = jnp.dot(xn_ref[...], w_ref[...], preferred_element_type=F32).astype(z_ref.dtype)


def _rms_inproj(h, g, w):
    t, d = h.shape
    n = w.shape[1]
    tm = _tile(t, 1024)
    tn = n // 2
    return pl.pallas_call(
        _rms_inproj_kernel,
        grid=(t // tm, n // tn),
        in_specs=[
            pl.BlockSpec((tm, d), lambda i, j: (i, 0)),
            pl.BlockSpec((1, d), lambda i, j: (0, 0)),
            pl.BlockSpec((d, tn), lambda i, j: (0, j)),
        ],
        out_specs=pl.BlockSpec((tm, tn), lambda i, j: (i, j)),
        out_shape=jax.ShapeDtypeStruct((t, n), BF16),
        scratch_shapes=[pltpu.VMEM((tm, d), BF16)],
        compiler_params=_params("parallel", "arbitrary"),
        name="rms_inproj",
    )(h, g.reshape(1, d), w)


def _pool_kernel(xa_ref, w_ref, scale_ref, y_ref, pad_ref):
    s = xa_ref.shape[0]
    for gi in range(len(POOL_WINDOWS)):
        pad_ref[gi, 0:HIST, :] = jnp.zeros((HIST, LANES), F32)
        pad_ref[gi, HIST:HIST + s, :] = xa_ref[:, gi * LANES:(gi + 1) * LANES].astype(F32)

    def chunk(c, carry):
        base = pl.multiple_of(c * ROW_CHUNK, ROW_CHUNK)
        count = (lax.broadcasted_iota(jnp.int32, (ROW_CHUNK, LANES), 0) + (base + 1)).astype(F32)
        for gi, win in enumerate(POOL_WINDOWS):
            cols = slice(gi * LANES, (gi + 1) * LANES)
            x = pad_ref[gi, pl.ds(base + HIST, ROW_CHUNK), :]
            wsum = x
            for k in range(1, win):
                wsum = wsum + pad_ref[gi, pl.ds(base + HIST - k, ROW_CHUNK), :]
            pooled = wsum / jnp.minimum(count, float(win)) - x
            y = jnp.dot(pooled.astype(BF16), w_ref[gi], preferred_element_type=F32)
            y_ref[pl.ds(base, ROW_CHUNK), cols] = (y * scale_ref[:, cols]).astype(y_ref.dtype)
        return carry

    lax.fori_loop(0, s // ROW_CHUNK, chunk, 0)


def _pool_mixer(z3, pool_w, pool_scale):
    b, s, _ = z3.shape
    ng = len(POOL_WINDOWS)
    width = ng * LANES
    return pl.pallas_call(
        _pool_kernel,
        grid=(b,),
        in_specs=[
            pl.BlockSpec((None, s, width), lambda i: (i, 0, 0)),
            pl.BlockSpec((ng, LANES, LANES), lambda i: (0, 0, 0)),
            pl.BlockSpec((1, width), lambda i: (0, 0)),
        ],
        out_specs=pl.BlockSpec((None, s, width), lambda i: (i, 0, 0)),
        out_shape=jax.ShapeDtypeStruct((b, s, width), BF16),
        scratch_shapes=[pltpu.VMEM((ng, HIST + s, LANES), F32)],
        compiler_params=_params("parallel"),
        name="pool_mixer",
    )(z3, pool_w, pool_scale.reshape(1, width))


def _sconv_kernel(h_ref, gb_ref, gc_ref, w_ref, y_ref, pad_ref):
    s = h_ref.shape[0]
    pad_ref[0:HIST, :] = jnp.zeros((HIST, LANES), F32)
    pad_ref[HIST:HIST + s, :] = gc_ref[...].astype(F32) * h_ref[...].astype(F32)

    def chunk(c, carry):
        base = pl.multiple_of(c * ROW_CHUNK, ROW_CHUNK)
        acc = jnp.zeros((ROW_CHUNK, LANES), F32)
        for k in range(SCONV_K):
            shift = SCONV_K - 1 - k
            acc = acc + pad_ref[pl.ds(base + HIST - shift, ROW_CHUNK), :] * w_ref[k:k + 1, :]
        gate = gb_ref[pl.ds(base, ROW_CHUNK), :].astype(F32)
        y_ref[pl.ds(base, ROW_CHUNK), :] = (gate * acc).astype(y_ref.dtype)
        return carry

    lax.fori_loop(0, s // ROW_CHUNK, chunk, 0)


def _sconv_mixer(z3, sconv_w, col0, nheads):
    b, s, _ = z3.shape
    blk0 = col0 // LANES

    def col(off):
        return pl.BlockSpec((None, s, LANES), lambda i, h: (i, 0, blk0 + off * nheads + h))

    return pl.pallas_call(
        _sconv_kernel,
        grid=(b, nheads),
        in_specs=[col(0), col(1), col(2), pl.BlockSpec((SCONV_K, LANES), lambda i, h: (0, h))],
        out_specs=pl.BlockSpec((None, s, LANES), lambda i, h: (i, 0, h)),
        out_shape=jax.ShapeDtypeStruct((b, s, nheads * LANES), BF16),
        scratch_shapes=[pltpu.VMEM((HIST + s, LANES), F32)],
        compiler_params=_params("parallel", "parallel"),
        name="sconv_mixer",
    )(z3, z3, z3, sconv_w)


def _conf_kernel(val_ref, gate_ref, w_ref, b_ref, lng_ref, lnb_ref, y_ref, pad_ref):
    s = val_ref.shape[0]
    pad_ref[0:HIST, :] = jnp.zeros((HIST, LANES), F32)
    pad_ref[HIST:HIST + s, :] = val_ref[...].astype(F32) * jax.nn.sigmoid(gate_ref[...].astype(F32))

    def chunk(c, carry):
        base = pl.multiple_of(c * ROW_CHUNK, ROW_CHUNK)
        acc = jnp.zeros((ROW_CHUNK, LANES), F32)
        for k in range(CONF_K):
            shift = CONF_K - 1 - k
            acc = acc + pad_ref[pl.ds(base + HIST - shift, ROW_CHUNK), :] * w_ref[k:k + 1, :]
        u = acc + b_ref[...]
        mu = jnp.mean(u, axis=-1, keepdims=True)
        uc = u - mu
        var = jnp.mean(uc * uc, axis=-1, keepdims=True)
        ln = uc * lax.rsqrt(var + EPS) * lng_ref[...] + lnb_ref[...]
        y_ref[pl.ds(base, ROW_CHUNK), :] = (ln * jax.nn.sigmoid(ln)).astype(y_ref.dtype)
        return carry

    lax.fori_loop(0, s // ROW_CHUNK, chunk, 0)


def _conf_mixer(z3, conf_w, conf_b, ln_g, ln_b, col0, nheads):
    b, s, _ = z3.shape
    blk0 = col0 // LANES
    width = nheads * LANES

    def col(off):
        return pl.BlockSpec((None, s, LANES), lambda i, h: (i, 0, blk0 + off * nheads + h))

    vec = pl.BlockSpec((1, LANES), lambda i, h: (0, h))
    return pl.pallas_call(
        _conf_kernel,
        grid=(b, nheads),
        in_specs=[col(0), col(1), pl.BlockSpec((CONF_K, LANES), lambda i, h: (0, h)), vec, vec, vec],
        out_specs=pl.BlockSpec((None, s, LANES), lambda i, h: (i, 0, h)),
        out_shape=jax.ShapeDtypeStruct((b, s, width), BF16),
        scratch_shapes=[pltpu.VMEM((HIST + s, LANES), F32)],
        compiler_params=_params("parallel", "parallel"),
        name="conf_mixer",
    )(z3, z3, conf_w, conf_b.reshape(1, width), ln_g.reshape(1, width), ln_b.reshape(1, width))


def _outproj_kernel(ya_ref, yb_ref, yc_ref, w_ref, h_ref, g_ref, h1_ref, xn_ref):
    wa, wb = ya_ref.shape[1], yb_ref.shape[1]
    acc = jnp.dot(ya_ref[...], w_ref[0:wa, :], preferred_element_type=F32)
    acc = acc + jnp.dot(yb_ref[...], w_ref[wa:wa + wb, :], preferred_element_type=F32)
    acc = acc + jnp.dot(yc_ref[...], w_ref[wa + wb:, :], preferred_element_type=F32)
    h1 = h_ref[...] + acc
    h1_ref[...] = h1
    ms = jnp.mean(h1 * h1, axis=-1, keepdims=True)
    xn_ref[...] = (h1 * lax.rsqrt(ms + EPS) * g_ref[...]).astype(BF16)


def _outproj(ya, yb, yc, w, h, g):
    t, d = h.shape
    tm = _tile(t, 512)

    def rows(width):
        return pl.BlockSpec((tm, width), lambda i: (i, 0))

    return pl.pallas_call(
        _outproj_kernel,
        grid=(t // tm,),
        in_specs=[rows(ya.shape[1]), rows(yb.shape[1]), rows(yc.shape[1]),
                  pl.BlockSpec(w.shape, lambda i: (0, 0)), rows(d), pl.BlockSpec((1, d), lambda i: (0, 0))],
        out_specs=[rows(d), rows(d)],
        out_shape=[jax.ShapeDtypeStruct((t, d), F32), jax.ShapeDtypeStruct((t, d), BF16)],
        compiler_params=_params("parallel"),
        name="outproj_norm",
    )(ya, yb, yc, w, h, g.reshape(1, d))


def _top_ranks(scores, vals_ref):
    key = lax.broadcasted_iota(jnp.int32, scores.shape, 0).astype(F32)

    def body(r, carry):
        sc, rank = carry
        m = jnp.max(sc, axis=0, keepdims=True)
        first = jnp.min(jnp.where(sc == m, key, float(PEER_NKEYS)), axis=0, keepdims=True)
        hit = key == first
        vals_ref[pl.ds(r, 1), :] = m
        return jnp.where(hit, -jnp.inf, sc), jnp.where(hit, r.astype(F32), rank)

    _, rank = lax.fori_loop(0, PEER_TOPK, body, (scores, jnp.full(scores.shape, float(PEER_TOPK), F32)))
    return rank


def _route_tile(s1, s2, t1_ref, t2_ref, cnt_ref):
    rank1 = _top_ranks(s1, t1_ref)
    rank2 = _top_ranks(s2, t2_ref)
    t1 = t1_ref[...]
    t2 = t2_ref[...]
    lo, hi = slice(0, SUBLANES), slice(SUBLANES, 2 * SUBLANES)

    blocks = [t1[0:1] + t2[lo], t1[0:1] + t2[hi]]
    blocks += [t1[i:i + 1] + t2[lo] for i in range(1, SUBLANES)]
    blocks += [t1[hi] + t2[0:1]]
    cand0 = jnp.concatenate(blocks, axis=0)
    nrow = cand0.shape[0]
    row = lax.broadcasted_iota(jnp.int32, (nrow, LANES), 0)
    blk = lax.shift_right_logical(row, 3).astype(F32)
    sub = lax.bitwise_and(row, SUBLANES - 1).astype(F32)
    ci = jnp.where(blk <= 1, 0.0, jnp.where(blk <= SUBLANES, blk - 1, SUBLANES + sub))
    cj = jnp.where(blk == 1, SUBLANES + sub, jnp.where(blk <= SUBLANES, sub, 0.0))
    flat = ci * PEER_TOPK + cj
    cand0 = jnp.where((ci + 1) * (cj + 1) <= PEER_TOPK, cand0, -jnp.inf)

    def body(r, carry):
        cand, sel = carry
        m = jnp.max(cand, axis=0, keepdims=True)
        first = jnp.min(jnp.where(cand == m, flat, float(PEER_TOPK * PEER_TOPK)), axis=0, keepdims=True)
        hit = flat == first
        return jnp.where(hit, -jnp.inf, cand), jnp.where(hit, 1.0, sel)

    _, sel = lax.fori_loop(0, PEER_TOPK, body, (cand0, jnp.zeros((nrow, LANES), F32)))

    cnt_ref[0:1, :] = jnp.sum(sel[0:2 * SUBLANES], axis=0, keepdims=True)
    for i in range(1, SUBLANES):
        cnt_ref[i:i + 1, :] = jnp.sum(sel[(i + 1) * SUBLANES:(i + 2) * SUBLANES], axis=0, keepdims=True)
    cnt_ref[hi, :] = sel[(SUBLANES + 1) * SUBLANES:]

    c1 = jnp.zeros(s1.shape, F32)
    for i in range(PEER_TOPK):
        c1 = jnp.where(rank1 == i, cnt_ref[i:i + 1, :], c1)

    top = t1[0:1] + t2[0:1]
    z = jnp.sum(jnp.where(sel > 0, jnp.exp(cand0 - top), 0.0), axis=0, keepdims=True)
    e1 = jnp.exp(s1 - t1[0:1])
    w2 = jnp.exp(s2 - t2[0:1]) / z
    return rank2, w2, c1, e1


def _route_kernel(xn_ref, wq_ref, k1_ref, k2_ref, r2_ref, w2_ref, c1_ref, e1_ref,
                  s1_ref, s2_ref, t1_ref, t2_ref, cnt_ref):
    tr = xn_ref.shape[0]
    q = jnp.dot(xn_ref[...], wq_ref[...], preferred_element_type=F32)
    nt = (((1,), (1,)), ((), ()))
    for h in range(PEER_HEADS):
        qa = q[:, (2 * h) * PEER_NKEYS:(2 * h + 1) * PEER_NKEYS].astype(BF16)
        qb = q[:, (2 * h + 1) * PEER_NKEYS:(2 * h + 2) * PEER_NKEYS].astype(BF16)
        s1_ref[h] = lax.dot_general(k1_ref[h], qa, nt, preferred_element_type=F32)
        s2_ref[h] = lax.dot_general(k2_ref[h], qb, nt, preferred_element_type=F32)

    nsub = tr // LANES

    def body(it, carry):
        h = it // nsub
        tok = pl.ds(pl.multiple_of((it % nsub) * LANES, LANES), LANES)
        r2, w2, c1, e1 = _route_tile(s1_ref[h, :, tok], s2_ref[h, :, tok], t1_ref, t2_ref, cnt_ref)
        r2_ref[h, :, tok] = r2
        w2_ref[h, :, tok] = w2
        c1_ref[h, :, tok] = c1
        e1_ref[h, :, tok] = e1
        return carry

    lax.fori_loop(0, PEER_HEADS * nsub, body, 0)


def _route(xn2, wq, k1, k2):
    t, d = xn2.shape
    tr = _tile(t, 256)
    plane = pl.BlockSpec((PEER_HEADS, PEER_NKEYS, tr), lambda i: (0, 0, i))
    shape = jax.ShapeDtypeStruct((PEER_HEADS, PEER_NKEYS, t), F32)
    keys = pl.BlockSpec(k1.shape, lambda i: (0, 0, 0))
    return pl.pallas_call(
        _route_kernel,
        grid=(t // tr,),
        in_specs=[pl.BlockSpec((tr, d), lambda i: (i, 0)), pl.BlockSpec(wq.shape, lambda i: (0, 0)), keys, keys],
        out_specs=[plane] * 4,
        out_shape=[shape] * 4,
        scratch_shapes=[
            pltpu.VMEM((PEER_HEADS, PEER_NKEYS, tr), F32),
            pltpu.VMEM((PEER_HEADS, PEER_NKEYS, tr), F32),
            pltpu.VMEM((PEER_TOPK, LANES), F32),
            pltpu.VMEM((PEER_TOPK, LANES), F32),
            pltpu.VMEM((PEER_TOPK, LANES), F32),
        ],
        compiler_params=_params("parallel"),
        name="peer_route",
    )(xn2, wq, k1, k2)


def _expert_kernel(xn_ref, u_ref, v_ref, r2_ref, w2_ref, c1_ref, e1_ref, h1_ref, *rest, final_norm):
    if final_norm:
        g_ref, o_ref, p_ref = rest
    else:
        o_ref, p_ref = rest
    j = pl.program_id(1)
    nkey1 = u_ref.shape[0] // PEER_NKEYS

    @pl.when(j == 0)
    def _():
        o_ref[...] = h1_ref[...]

    nt = (((1,), (1,)), ((), ()))
    pre = lax.dot_general(u_ref[...], xn_ref[...], nt, preferred_element_type=F32)
    act = 0.5 * pre * (1.0 + lax.erf(pre * (0.5 ** 0.5)))
    for a in range(nkey1):
        n1 = j * nkey1 + a
        gate = jnp.zeros((PEER_NKEYS, xn_ref.shape[0]), F32)
        for h in range(PEER_HEADS):
            c1 = c1_ref[h, pl.ds(n1, 1), :]
            e1 = e1_ref[h, pl.ds(n1, 1), :]
            gate = gate + jnp.where(r2_ref[h] < c1, w2_ref[h], 0.0) * e1
        rows = slice(a * PEER_NKEYS, (a + 1) * PEER_NKEYS)
        p_ref[rows, :] = (act[rows, :] * gate).astype(BF16)
    tn = (((0,), (0,)), ((), ()))
    o_ref[...] += lax.dot_general(p_ref[...], v_ref[...], tn, preferred_element_type=F32)

    if final_norm:
        @pl.when(j == pl.num_programs(1) - 1)
        def _():
            y = o_ref[...]
            ms = jnp.mean(y * y, axis=-1, keepdims=True)
            o_ref[...] = y * lax.rsqrt(ms + EPS) * g_ref[...]


def _experts(xn2, u, v, r2, w2, c1, e1, h1, final_g):
    t, d = xn2.shape
    ne = u.shape[0]
    tm = _tile(t, 512)
    ec = 512
    rows = pl.BlockSpec((tm, d), lambda i, j: (i, 0))
    chunk = pl.BlockSpec((ec, d), lambda i, j: (j, 0))
    plane = pl.BlockSpec((PEER_HEADS, PEER_NKEYS, tm), lambda i, j: (0, 0, i))
    in_specs = [rows, chunk, chunk, plane, plane, plane, plane, rows]
    args = [xn2, u, v, r2, w2, c1, e1, h1]
    if final_g is not None:
        in_specs.append(pl.BlockSpec((1, d), lambda i, j: (0, 0)))
        args.append(final_g.reshape(1, d))
    return pl.pallas_call(
        functools.partial(_expert_kernel, final_norm=final_g is not None),
        grid=(t // tm, ne // ec),
        in_specs=in_specs,
        out_specs=rows,
        out_shape=jax.ShapeDtypeStruct((t, d), F32),
        scratch_shapes=[pltpu.VMEM((ec, tm), BF16)],
        compiler_params=_params("parallel", "arbitrary"),
        name="peer_experts",
    )(*args)


def kernel(x, norm1_g, w_in, pool_w, pool_scale, sconv_w, conf_w, conf_b, conf_ln_g, conf_ln_b, w_out,
           norm2_g, peer_wq, peer_k1, peer_k2, peer_u, peer_v, final_g):
    b, s, d = x.shape
    depth = w_in.shape[0]
    pool_width = pool_scale.shape[1]
    sconv_width = sconv_w.shape[2]
    conf_width = conf_w.shape[2]
    assert pool_width == len(POOL_WINDOWS) * LANES and s % ROW_CHUNK == 0
    assert peer_k1.shape[1:] == (PEER_HEADS, PEER_NKEYS, PEER_NKEYS)
    h = x.reshape(b * s, d)
    for l in range(depth):
        last = l == depth - 1
        z = _rms_inproj(h, norm1_g[l], w_in[l].astype(BF16))
        z3 = z.reshape(b, s, z.shape[1])
        ya = _pool_mixer(z3, pool_w[l].astype(BF16), pool_scale[l])
        yb = _sconv_mixer(z3, sconv_w[l], pool_width, sconv_width // LANES)
        yc = _conf_mixer(z3, conf_w[l], conf_b[l], conf_ln_g[l], conf_ln_b[l],
                         pool_width + 3 * sconv_width, conf_width // LANES)
        h1, xn2 = _outproj(ya.reshape(b * s, -1), yb.reshape(b * s, -1), yc.reshape(b * s, -1),
                           w_out[l].astype(BF16), h, norm2_g[l])
        r2, w2, c1, e1 = _route(xn2, peer_wq[l].astype(BF16), peer_k1[l].astype(BF16), peer_k2[l].astype(BF16))
        h = _experts(xn2, peer_u[l].astype(BF16), peer_v[l].astype(BF16), r2, w2, c1, e1, h1,
                     final_g if last else None)
    return h.reshape(b, s, d)
```

```python
import functools

import jax
import jax.numpy as jnp
from jax import lax
from jax.experimental import pallas as pl
from jax.experimental.pallas import tpu as pltpu

F32 = jnp.float32
BF16 = jnp.bfloat16

EPS = 1e-6
LANES = 128
SUBLANES = 8
POOL_WINDOWS = (2, 4, 8, 16)
SCONV_K = 3
CONF_K = 31
PEER_HEADS = 8
PEER_NKEYS = 128
PEER_TOPK = 16
HIST = 32
ROW_CHUNK = 256
BF16_ROWS = 2 * SUBLANES
EXPERT_SUB = 512
VMEM_LIMIT = 56 * 1024 * 1024


def _params(*sem):
    return pltpu.CompilerParams(dimension_semantics=sem, vmem_limit_bytes=VMEM_LIMIT)


def _tile(n, want):
    t = min(n, want)
    assert n % t == 0, (n, want)
    return t


def _pack(x):
    return pltpu.bitcast(x.astype(BF16), jnp.uint32)


def _unpack(w):
    return pltpu.bitcast(w, BF16)


def _pack_kernel(x_ref, o_ref):
    o_ref[...] = _pack(x_ref[...])


def _pack_rows(x):
    r, c = x.shape
    tr = _tile(r, 1024)
    return pl.pallas_call(
        _pack_kernel,
        grid=(r // tr,),
        in_specs=[pl.BlockSpec((tr, c), lambda i: (i, 0))],
        out_specs=pl.BlockSpec((tr // 2, c), lambda i: (i, 0)),
        out_shape=jax.ShapeDtypeStruct((r // 2, c), jnp.uint32),
        compiler_params=_params("parallel"),
        name="pack_rows",
    )(x)


def _rms_inproj_kernel(x_ref, g_ref, w_ref, z_ref, xn_ref):
    @pl.when(pl.program_id(1) == 0)
    def _():
        x = x_ref[...]
        ms = jnp.mean(x * x, axis=-1, keepdims=True)
        xn_ref[...] = (x * lax.rsqrt(ms + EPS) * g_ref[...]).astype(BF16)

    z_ref[...] = jnp.dot(xn_ref[...], w_ref[...], preferred_element_type=F32).astype(z_ref.dtype)


def _rms_inproj(h, g, w):
    t, d = h.shape
    n = w.shape[1]
    tm = _tile(t, 1024)
    tn = n // 2
    return pl.pallas_call(
        _rms_inproj_kernel,
        grid=(t // tm, n // tn),
        in_specs=[
            pl.BlockSpec((tm, d), lambda i, j: (i, 0)),
            pl.BlockSpec((1, d), lambda i, j: (0, 0)),
            pl.BlockSpec((d, tn), lambda i, j: (0, j)),
        ],
        out_specs=pl.BlockSpec((tm, tn), lambda i, j: (i, j)),
        out_shape=jax.ShapeDtypeStruct((t, n), BF16),
        scratch_shapes=[pltpu.VMEM((tm, d), BF16)],
        compiler_params=_params("parallel", "arbitrary"),
        name="rms_inproj",
    )(h, g.reshape(1, d), w)


def _pool_kernel(xa_ref, w_ref, scale_ref, y_ref, pad_ref):
    s = xa_ref.shape[0]
    for gi in range(len(POOL_WINDOWS)):
        pad_ref[gi, 0:HIST, :] = jnp.zeros((HIST, LANES), F32)
        pad_ref[gi, HIST:HIST + s, :] = xa_ref[:, gi * LANES:(gi + 1) * LANES].astype(F32)

    def chunk(c, carry):
        base = pl.multiple_of(c * ROW_CHUNK, ROW_CHUNK)
        count = (lax.broadcasted_iota(jnp.int32, (ROW_CHUNK, LANES), 0) + (base + 1)).astype(F32)
        for gi, win in enumerate(POOL_WINDOWS):
            cols = slice(gi * LANES, (gi + 1) * LANES)
            x = pad_ref[gi, pl.ds(base + HIST, ROW_CHUNK), :]
            wsum = x
            for k in range(1, win):
                wsum = wsum + pad_ref[gi, pl.ds(base + HIST - k, ROW_CHUNK), :]
            pooled = wsum / jnp.minimum(count, float(win)) - x
            y = jnp.dot(pooled.astype(BF16), w_ref[gi], preferred_element_type=F32)
            y_ref[pl.ds(base, ROW_CHUNK), cols] = (y * scale_ref[:, cols]).astype(y_ref.dtype)
        return carry

    lax.fori_loop(0, s // ROW_CHUNK, chunk, 0)


def _pool_mixer(z3, pool_w, pool_scale):
    b, s, _ = z3.shape
    ng = len(POOL_WINDOWS)
    width = ng * LANES
    return pl.pallas_call(
        _pool_kernel,
        grid=(b,),
        in_specs=[
            pl.BlockSpec((None, s, width), lambda i: (i, 0, 0)),
            pl.BlockSpec((ng, LANES, LANES), lambda i: (0, 0, 0)),
            pl.BlockSpec((1, width), lambda i: (0, 0)),
        ],
        out_specs=pl.BlockSpec((None, s, width), lambda i: (i, 0, 0)),
        out_shape=jax.ShapeDtypeStruct((b, s, width), BF16),
        scratch_shapes=[pltpu.VMEM((ng, HIST + s, LANES), F32)],
        compiler_params=_params("parallel"),
        name="pool_mixer",
    )(z3, pool_w, pool_scale.reshape(1, width))


def _sconv_kernel(h_ref, gb_ref, gc_ref, w_ref, y_ref, pad_ref):
    s = h_ref.shape[0]
    pad_ref[0:HIST, :] = jnp.zeros((HIST, LANES), F32)
    pad_ref[HIST:HIST + s, :] = gc_ref[...].astype(F32) * h_ref[...].astype(F32)

    def chunk(c, carry):
        base = pl.multiple_of(c * ROW_CHUNK, ROW_CHUNK)
        acc = jnp.zeros((ROW_CHUNK, LANES), F32)
        for k in range(SCONV_K):
            shift = SCONV_K - 1 - k
            acc = acc + pad_ref[pl.ds(base + HIST - shift, ROW_CHUNK), :] * w_ref[k:k + 1, :]
        gate = gb_ref[pl.ds(base, ROW_CHUNK), :].astype(F32)
        y_ref[pl.ds(base, ROW_CHUNK), :] = (gate * acc).astype(y_ref.dtype)
        return carry

    lax.fori_loop(0, s // ROW_CHUNK, chunk, 0)


def _sconv_mixer(z3, sconv_w, col0, nheads):
    b, s, _ = z3.shape
    blk0 = col0 // LANES

    def col(off):
        return pl.BlockSpec((None, s, LANES), lambda i, h: (i, 0, blk0 + off * nheads + h))

    return pl.pallas_call(
        _sconv_kernel,
        grid=(b, nheads),
        in_specs=[col(0), col(1), col(2), pl.BlockSpec((SCONV_K, LANES), lambda i, h: (0, h))],
        out_specs=pl.BlockSpec((None, s, LANES), lambda i, h: (i, 0, h)),
        out_shape=jax.ShapeDtypeStruct((b, s, nheads * LANES), BF16),
        scratch_shapes=[pltpu.VMEM((HIST + s, LANES), F32)],
        compiler_params=_params("parallel", "parallel"),
        name="sconv_mixer",
    )(z3, z3, z3, sconv_w)


def _conf_kernel(val_ref, gate_ref, w_ref, b_ref, lng_ref, lnb_ref, y_ref, pad_ref):
    s = val_ref.shape[0]
    pad_ref[0:HIST, :] = jnp.zeros((HIST, LANES), F32)
    pad_ref[HIST:HIST + s, :] = val_ref[...].astype(F32) * jax.nn.sigmoid(gate_ref[...].astype(F32))

    def chunk(c, carry):
        base = pl.multiple_of(c * ROW_CHUNK, ROW_CHUNK)
        acc = jnp.zeros((ROW_CHUNK, LANES), F32)
        for k in range(CONF_K):
            shift = CONF_K - 1 - k
            acc = acc + pad_ref[pl.ds(base + HIST - shift, ROW_CHUNK), :] * w_ref[k:k + 1, :]
        u = acc + b_ref[...]
        mu = jnp.mean(u, axis=-1, keepdims=True)
        uc = u - mu
        var = jnp.mean(uc * uc, axis=-1, keepdims=True)
        ln = uc * lax.rsqrt(var + EPS) * lng_ref[...] + lnb_ref[...]
        y_ref[pl.ds(base, ROW_CHUNK), :] = (ln * jax.nn.sigmoid(ln)).astype(y_ref.dtype)
        return carry

    lax.fori_loop(0, s // ROW_CHUNK, chunk, 0)


def _conf_mixer(z3, conf_w, conf_b, ln_g, ln_b, col0, nheads):
    b, s, _ = z3.shape
    blk0 = col0 // LANES
    width = nheads * LANES

    def col(off):
        return pl.BlockSpec((None, s, LANES), lambda i, h: (i, 0, blk0 + off * nheads + h))

    vec = pl.BlockSpec((1, LANES), lambda i, h: (0, h))
    return pl.pallas_call(
        _conf_kernel,
        grid=(b, nheads),
        in_specs=[col(0), col(1), pl.BlockSpec((CONF_K, LANES), lambda i, h: (0, h)), vec, vec, vec],
        out_specs=pl.BlockSpec((None, s, LANES), lambda i, h: (i, 0, h)),
        out_shape=jax.ShapeDtypeStruct((b, s, width), BF16),
        scratch_shapes=[pltpu.VMEM((HIST + s, LANES), F32)],
        compiler_params=_params("parallel", "parallel"),
        name="conf_mixer",
    )(z3, z3, conf_w, conf_b.reshape(1, width), ln_g.reshape(1, width), ln_b.reshape(1, width))


def _outproj_kernel(ya_ref, yb_ref, yc_ref, w_ref, h_ref, g_ref, h1_ref, xn_ref):
    wa, wb = ya_ref.shape[1], yb_ref.shape[1]
    acc = jnp.dot(ya_ref[...], w_ref[0:wa, :], preferred_element_type=F32)
    acc = acc + jnp.dot(yb_ref[...], w_ref[wa:wa + wb, :], preferred_element_type=F32)
    acc = acc + jnp.dot(yc_ref[...], w_ref[wa + wb:, :], preferred_element_type=F32)
    h1 = h_ref[...] + acc
    h1_ref[...] = h1
    ms = jnp.mean(h1 * h1, axis=-1, keepdims=True)
    xn_ref[...] = _pack(h1 * lax.rsqrt(ms + EPS) * g_ref[...])


def _outproj(ya, yb, yc, w, h, g):
    t, d = h.shape
    tm = _tile(t, 512)

    def rows(width):
        return pl.BlockSpec((tm, width), lambda i: (i, 0))

    return pl.pallas_call(
        _outproj_kernel,
        grid=(t // tm,),
        in_specs=[rows(ya.shape[1]), rows(yb.shape[1]), rows(yc.shape[1]),
                  pl.BlockSpec(w.shape, lambda i: (0, 0)), rows(d), pl.BlockSpec((1, d), lambda i: (0, 0))],
        out_specs=[rows(d), pl.BlockSpec((tm // 2, d), lambda i: (i, 0))],
        out_shape=[jax.ShapeDtypeStruct((t, d), F32), jax.ShapeDtypeStruct((t // 2, d), jnp.uint32)],
        compiler_params=_params("parallel"),
        name="outproj_norm",
    )(ya, yb, yc, w, h, g.reshape(1, d))


def _top_ranks(scores, vals_ref):
    key = lax.broadcasted_iota(jnp.int32, scores.shape, 0).astype(F32)

    def body(r, carry):
        sc, rank = carry
        m = jnp.max(sc, axis=0, keepdims=True)
        first = jnp.min(jnp.where(sc == m, key, float(PEER_NKEYS)), axis=0, keepdims=True)
        hit = key == first
        vals_ref[pl.ds(r, 1), :] = m
        return jnp.where(hit, -jnp.inf, sc), jnp.where(hit, r.astype(F32), rank)

    _, rank = lax.fori_loop(0, PEER_TOPK, body, (scores, jnp.full(scores.shape, float(PEER_TOPK), F32)))
    return rank


def _route_tile(s1, s2, t1_ref, t2_ref, cnt_ref):
    rank1 = _top_ranks(s1, t1_ref)
    rank2 = _top_ranks(s2, t2_ref)
    t1 = t1_ref[...]
    t2 = t2_ref[...]
    lo, hi = slice(0, SUBLANES), slice(SUBLANES, 2 * SUBLANES)

    blocks = [t1[0:1] + t2[lo], t1[0:1] + t2[hi]]
    blocks += [t1[i:i + 1] + t2[lo] for i in range(1, SUBLANES)]
    blocks += [t1[hi] + t2[0:1]]
    cand0 = jnp.concatenate(blocks, axis=0)
    nrow = cand0.shape[0]
    row = lax.broadcasted_iota(jnp.int32, (nrow, LANES), 0)
    blk = lax.shift_right_logical(row, 3).astype(F32)
    sub = lax.bitwise_and(row, SUBLANES - 1).astype(F32)
    ci = jnp.where(blk <= 1, 0.0, jnp.where(blk <= SUBLANES, blk - 1, SUBLANES + sub))
    cj = jnp.where(blk == 1, SUBLANES + sub, jnp.where(blk <= SUBLANES, sub, 0.0))
    flat = ci * PEER_TOPK + cj
    cand0 = jnp.where((ci + 1) * (cj + 1) <= PEER_TOPK, cand0, -jnp.inf)

    def body(r, carry):
        cand, sel = carry
        m = jnp.max(cand, axis=0, keepdims=True)
        first = jnp.min(jnp.where(cand == m, flat, float(PEER_TOPK * PEER_TOPK)), axis=0, keepdims=True)
        hit = flat == first
        return jnp.where(hit, -jnp.inf, cand), jnp.where(hit, 1.0, sel)

    _, sel = lax.fori_loop(0, PEER_TOPK, body, (cand0, jnp.zeros((nrow, LANES), F32)))

    cnt_ref[0:1, :] = jnp.sum(sel[0:2 * SUBLANES], axis=0, keepdims=True)
    for i in range(1, SUBLANES):
        cnt_ref[i:i + 1, :] = jnp.sum(sel[(i + 1) * SUBLANES:(i + 2) * SUBLANES], axis=0, keepdims=True)
    cnt_ref[hi, :] = sel[(SUBLANES + 1) * SUBLANES:]

    c1 = jnp.zeros(s1.shape, F32)
    for i in range(PEER_TOPK):
        c1 = jnp.where(rank1 == i, cnt_ref[i:i + 1, :], c1)

    top = t1[0:1] + t2[0:1]
    z = jnp.sum(jnp.where(sel > 0, jnp.exp(cand0 - top), 0.0), axis=0, keepdims=True)
    e1 = jnp.exp(s1 - t1[0:1])
    w2 = jnp.exp(s2 - t2[0:1]) / z
    return rank2, w2, c1, e1


def _route_kernel(xn_ref, wq_ref, k1_ref, k2_ref, r2_ref, w2_ref, c1_ref, e1_ref,
                  s1_ref, s2_ref, t1_ref, t2_ref, cnt_ref):
    tr = 2 * xn_ref.shape[0]
    q = jnp.dot(_unpack(xn_ref[...]), wq_ref[...], preferred_element_type=F32)
    nt = (((1,), (1,)), ((), ()))
    for h in range(PEER_HEADS):
        qa = q[:, (2 * h) * PEER_NKEYS:(2 * h + 1) * PEER_NKEYS].astype(BF16)
        qb = q[:, (2 * h + 1) * PEER_NKEYS:(2 * h + 2) * PEER_NKEYS].astype(BF16)
        s1_ref[h] = lax.dot_general(k1_ref[h], qa, nt, preferred_element_type=F32)
        s2_ref[h] = lax.dot_general(k2_ref[h], qb, nt, preferred_element_type=F32)

    nsub = tr // LANES

    def body(it, carry):
        h = it // nsub
        tok = pl.ds(pl.multiple_of((it % nsub) * LANES, LANES), LANES)
        r2, w2, c1, e1 = _route_tile(s1_ref[h, :, tok], s2_ref[h, :, tok], t1_ref, t2_ref, cnt_ref)
        r2_ref[h, :, tok] = r2.astype(r2_ref.dtype)
        w2_ref[h, :, tok] = w2.astype(w2_ref.dtype)
        c1_ref[h, :, tok] = c1
        e1_ref[h, :, tok] = e1
        return carry

    lax.fori_loop(0, PEER_HEADS * nsub, body, 0)


def _route(xn2, wq, k1, k2):
    t, d = 2 * xn2.shape[0], xn2.shape[1]
    tr = _tile(t, 256)
    plane = pl.BlockSpec((PEER_HEADS, PEER_NKEYS, tr), lambda i: (0, 0, i))
    dense = jax.ShapeDtypeStruct((PEER_HEADS, PEER_NKEYS, t), BF16)
    byrow = jax.ShapeDtypeStruct((PEER_HEADS, PEER_NKEYS, t), F32)
    keys = pl.BlockSpec(k1.shape, lambda i: (0, 0, 0))
    return pl.pallas_call(
        _route_kernel,
        grid=(t // tr,),
        in_specs=[pl.BlockSpec((tr // 2, d), lambda i: (i, 0)), pl.BlockSpec(wq.shape, lambda i: (0, 0)), keys, keys],
        out_specs=[plane] * 4,
        out_shape=[dense, dense, byrow, byrow],
        scratch_shapes=[
            pltpu.VMEM((PEER_HEADS, PEER_NKEYS, tr), F32),
            pltpu.VMEM((PEER_HEADS, PEER_NKEYS, tr), F32),
            pltpu.VMEM((PEER_TOPK, LANES), F32),
            pltpu.VMEM((PEER_TOPK, LANES), F32),
            pltpu.VMEM((PEER_TOPK, LANES), F32),
        ],
        compiler_params=_params("parallel"),
        name="peer_route",
    )(xn2, wq, k1, k2)


def _expert_kernel(xn_ref, u_ref, v_ref, r2_ref, w2_ref, c1_ref, e1_ref, h1_ref, *rest, final_norm):
    if final_norm:
        g_ref, o_ref, pre_ref, p_ref = rest
    else:
        o_ref, pre_ref, p_ref = rest
    j = pl.program_id(1)
    tm = 2 * xn_ref.shape[0]
    half = EXPERT_SUB // 2
    nsub = u_ref.shape[0] // half
    nkey1 = EXPERT_SUB // PEER_NKEYS
    nblk = PEER_NKEYS // BF16_ROWS

    @pl.when(j == 0)
    def _():
        o_ref[...] = h1_ref[...]

    nt = (((1,), (1,)), ((), ()))
    tn = (((0,), (0,)), ((), ()))
    xn = _unpack(xn_ref[...])
    for s in range(nsub):
        pre_ref[s] = lax.dot_general(_unpack(u_ref[s * half:(s + 1) * half, :]), xn, nt,
                                     preferred_element_type=F32)
    for s in range(nsub):
        for a in range(nkey1):
            n1 = (j * nsub + s) * nkey1 + a
            gate = [None] * nblk
            for h in range(PEER_HEADS):
                c1 = jnp.broadcast_to(c1_ref[h, pl.ds(n1, 1), :], (BF16_ROWS, tm)).astype(BF16)
                e1 = jnp.broadcast_to(e1_ref[h, pl.ds(n1, 1), :], (BF16_ROWS, tm)).astype(BF16)
                for rb in range(nblk):
                    rows = slice(rb * BF16_ROWS, (rb + 1) * BF16_ROWS)
                    g = jnp.where(r2_ref[h, rows, :] < c1, w2_ref[h, rows, :], jnp.zeros((), BF16)) * e1
                    gate[rb] = g if gate[rb] is None else gate[rb] + g
            for rb in range(nblk):
                rows = slice(a * PEER_NKEYS + rb * BF16_ROWS, a * PEER_NKEYS + (rb + 1) * BF16_ROWS)
                pre = pre_ref[s, rows, :]
                act = 0.5 * pre * (1.0 + lax.erf(pre * (0.5 ** 0.5)))
                p_ref[s, rows, :] = act.astype(BF16) * gate[rb]
        o_ref[...] += lax.dot_general(p_ref[s], v_ref[s * EXPERT_SUB:(s + 1) * EXPERT_SUB, :], tn,
                                      preferred_element_type=F32)

    if final_norm:
        @pl.when(j == pl.num_programs(1) - 1)
        def _():
            y = o_ref[...]
            ms = jnp.mean(y * y, axis=-1, keepdims=True)
            o_ref[...] = y * lax.rsqrt(ms + EPS) * g_ref[...]


def _experts(xn2, u, v, r2, w2, c1, e1, h1, final_g):
    t, d = h1.shape
    ne = v.shape[0]
    tm = _tile(t, 512)
    ec = 2 * EXPERT_SUB
    rows = pl.BlockSpec((tm, d), lambda i, j: (i, 0))
    chunk = pl.BlockSpec((ec, d), lambda i, j: (j, 0))
    plane = pl.BlockSpec((PEER_HEADS, PEER_NKEYS, tm), lambda i, j: (0, 0, i))
    in_specs = [pl.BlockSpec((tm // 2, d), lambda i, j: (i, 0)), pl.BlockSpec((ec // 2, d), lambda i, j: (j, 0)),
                chunk, plane, plane, plane, plane,
                pl.BlockSpec((tm, d), lambda i, j: (i, 0), pipeline_mode=pl.Buffered(1))]
    args = [xn2, u, v, r2, w2, c1, e1, h1]
    if final_g is not None:
        in_specs.append(pl.BlockSpec((1, d), lambda i, j: (0, 0)))
        args.append(final_g.reshape(1, d))
    return pl.pallas_call(
        functools.partial(_expert_kernel, final_norm=final_g is not None),
        grid=(t // tm, ne // ec),
        in_specs=in_specs,
        out_specs=rows,
        out_shape=jax.ShapeDtypeStruct((t, d), F32),
        scratch_shapes=[pltpu.VMEM((ec // EXPERT_SUB, EXPERT_SUB, tm), F32),
                        pltpu.VMEM((ec // EXPERT_SUB, EXPERT_SUB, tm), BF16)],
        compiler_params=_params("parallel", "arbitrary"),
        name="peer_experts",
    )(*args)


def kernel(x, norm1_g, w_in, pool_w, pool_scale, sconv_w, conf_w, conf_b, conf_ln_g, conf_ln_b, w_out,
           norm2_g, peer_wq, peer_k1, peer_k2, peer_u, peer_v, final_g):
    b, s, d = x.shape
    depth = w_in.shape[0]
    pool_width = pool_scale.shape[1]
    sconv_width = sconv_w.shape[2]
    conf_width = conf_w.shape[2]
    assert pool_width == len(POOL_WINDOWS) * LANES and s % ROW_CHUNK == 0
    assert peer_k1.shape[1:] == (PEER_HEADS, PEER_NKEYS, PEER_NKEYS)
    h = x.reshape(b * s, d)
    for l in range(depth):
        last = l == depth - 1
        z = _rms_inproj(h, norm1_g[l], w_in[l].astype(BF16))
        z3 = z.reshape(b, s, z.shape[1])
        ya = _pool_mixer(z3, pool_w[l].astype(BF16), pool_scale[l])
        yb = _sconv_mixer(z3, sconv_w[l], pool_width, sconv_width // LANES)
        yc = _conf_mixer(z3, conf_w[l], conf_b[l], conf_ln_g[l], conf_ln_b[l],
                         pool_width + 3 * sconv_width, conf_width // LANES)
        h1, xn2 = _outproj(ya.reshape(b * s, -1), yb.reshape(b * s, -1), yc.reshape(b * s, -1),
                           w_out[l].astype(BF16), h, norm2_g[l])
        r2, w2, c1, e1 = _route(xn2, peer_wq[l].astype(BF16), peer_k1[l].astype(BF16), peer_k2[l].astype(BF16))
        h = _experts(xn2, _pack_rows(peer_u[l]), peer_v[l].astype(BF16), r2, w2, c1, e1, h1,
                     final_g if last else None)
    return h.reshape(b, s, d)
```

```python
import functools

import jax
import jax.numpy as jnp
from jax import lax
from jax.experimental import pallas as pl
from jax.experimental.pallas import tpu as pltpu

F32 = jnp.float32
BF16 = jnp.bfloat16

EPS = 1e-6
LANES = 128
SUBLANES = 8
POOL_WINDOWS = (2, 4, 8, 16)
SCONV_K = 3
CONF_K = 31
PEER_HEADS = 8
PEER_NKEYS = 128
PEER_TOPK = 16
HIST = 32
ROW_CHUNK = 256
BF16_ROWS = 2 * SUBLANES
EXPERT_SUB = 512
VMEM_LIMIT = 56 * 1024 * 1024


def _params(*sem):
    return pltpu.CompilerParams(dimension_semantics=sem, vmem_limit_bytes=VMEM_LIMIT)


def _tile(n, want):
    t = min(n, want)
    assert n % t == 0, (n, want)
    return t


def _pack(x):
    return pltpu.bitcast(x.astype(BF16), jnp.uint32)


def _unpack(w):
    return pltpu.bitcast(w, BF16)


def _pack_kernel(x_ref, o_ref):
    o_ref[...] = _pack(x_ref[...])


def _pack_rows(x):
    r, c = x.shape
    tr = _tile(r, 1024)
    return pl.pallas_call(
        _pack_kernel,
        grid=(r // tr,),
        in_specs=[pl.BlockSpec((tr, c), lambda i: (i, 0))],
        out_specs=pl.BlockSpec((tr // 2, c), lambda i: (i, 0)),
        out_shape=jax.ShapeDtypeStruct((r // 2, c), jnp.uint32),
        compiler_params=_params("parallel"),
        name="pack_rows",
    )(x)


def _rms_inproj_kernel(x_ref, g_ref, w_ref, z_ref, xn_ref):
    @pl.when(pl.program_id(1) == 0)
    def _():
        x = x_ref[...]
        ms = jnp.mean(x * x, axis=-1, keepdims=True)
        xn_ref[...] = (x * lax.rsqrt(ms + EPS) * g_ref[...]).astype(BF16)

    z_ref[...] = jnp.dot(xn_ref[...], w_ref[...], preferred_element_type=F32).astype(z_ref.dtype)


def _rms_inproj(h, g, w):
    t, d = h.shape
    n = w.shape[1]
    tm = _tile(t, 1024)
    tn = n // 2
    return pl.pallas_call(
        _rms_inproj_kernel,
        grid=(t // tm, n // tn),
        in_specs=[
            pl.BlockSpec((tm, d), lambda i, j: (i, 0)),
            pl.BlockSpec((1, d), lambda i, j: (0, 0)),
            pl.BlockSpec((d, tn), lambda i, j: (0, j)),
        ],
        out_specs=pl.BlockSpec((tm, tn), lambda i, j: (i, j)),
        out_shape=jax.ShapeDtypeStruct((t, n), BF16),
        scratch_shapes=[pltpu.VMEM((tm, d), BF16)],
        compiler_params=_params("parallel", "arbitrary"),
        name="rms_inproj",
    )(h, g.reshape(1, d), w)


def _pool_kernel(xa_ref, w_ref, scale_ref, y_ref, pad_ref):
    s = xa_ref.shape[0]
    for gi in range(len(POOL_WINDOWS)):
        pad_ref[gi, 0:HIST, :] = jnp.zeros((HIST, LANES), F32)
        pad_ref[gi, HIST:HIST + s, :] = xa_ref[:, gi * LANES:(gi + 1) * LANES].astype(F32)

    def chunk(c, carry):
        base = pl.multiple_of(c * ROW_CHUNK, ROW_CHUNK)
        count = (lax.broadcasted_iota(jnp.int32, (ROW_CHUNK, LANES), 0) + (base + 1)).astype(F32)
        for gi, win in enumerate(POOL_WINDOWS):
            cols = slice(gi * LANES, (gi + 1) * LANES)
            x = pad_ref[gi, pl.ds(base + HIST, ROW_CHUNK), :]
            wsum = x
            for k in range(1, win):
                wsum = wsum + pad_ref[gi, pl.ds(base + HIST - k, ROW_CHUNK), :]
            pooled = wsum / jnp.minimum(count, float(win)) - x
            y = jnp.dot(pooled.astype(BF16), w_ref[gi], preferred_element_type=F32)
            y_ref[pl.ds(base, ROW_CHUNK), cols] = (y * scale_ref[:, cols]).astype(y_ref.dtype)
        return carry

    lax.fori_loop(0, s // ROW_CHUNK, chunk, 0)


def _pool_mixer(z3, pool_w, pool_scale):
    b, s, _ = z3.shape
    ng = len(POOL_WINDOWS)
    width = ng * LANES
    return pl.pallas_call(
        _pool_kernel,
        grid=(b,),
        in_specs=[
            pl.BlockSpec((None, s, width), lambda i: (i, 0, 0)),
            pl.BlockSpec((ng, LANES, LANES), lambda i: (0, 0, 0)),
            pl.BlockSpec((1, width), lambda i: (0, 0)),
        ],
        out_specs=pl.BlockSpec((None, s, width), lambda i: (i, 0, 0)),
        out_shape=jax.ShapeDtypeStruct((b, s, width), BF16),
        scratch_shapes=[pltpu.VMEM((ng, HIST + s, LANES), F32)],
        compiler_params=_params("parallel"),
        name="pool_mixer",
    )(z3, pool_w, pool_scale.reshape(1, width))


def _sconv_kernel(h_ref, gb_ref, gc_ref, w_ref, y_ref, pad_ref):
    s = h_ref.shape[0]
    pad_ref[0:HIST, :] = jnp.zeros((HIST, LANES), F32)
    pad_ref[HIST:HIST + s, :] = gc_ref[...].astype(F32) * h_ref[...].astype(F32)

    def chunk(c, carry):
        base = pl.multiple_of(c * ROW_CHUNK, ROW_CHUNK)
        acc = jnp.zeros((ROW_CHUNK, LANES), F32)
        for k in range(SCONV_K):
            shift = SCONV_K - 1 - k
            acc = acc + pad_ref[pl.ds(base + HIST - shift, ROW_CHUNK), :] * w_ref[k:k + 1, :]
        gate = gb_ref[pl.ds(base, ROW_CHUNK), :].astype(F32)
        y_ref[pl.ds(base, ROW_CHUNK), :] = (gate * acc).astype(y_ref.dtype)
        return carry

    lax.fori_loop(0, s // ROW_CHUNK, chunk, 0)


def _sconv_mixer(z3, sconv_w, col0, nheads):
    b, s, _ = z3.shape
    blk0 = col0 // LANES

    def col(off):
        return pl.BlockSpec((None, s, LANES), lambda i, h: (i, 0, blk0 + off * nheads + h))

    return pl.pallas_call(
        _sconv_kernel,
        grid=(b, nheads),
        in_specs=[col(0), col(1), col(2), pl.BlockSpec((SCONV_K, LANES), lambda i, h: (0, h))],
        out_specs=pl.BlockSpec((None, s, LANES), lambda i, h: (i, 0, h)),
        out_shape=jax.ShapeDtypeStruct((b, s, nheads * LANES), BF16),
        scratch_shapes=[pltpu.VMEM((HIST + s, LANES), F32)],
        compiler_params=_params("parallel", "parallel"),
        name="sconv_mixer",
    )(z3, z3, z3, sconv_w)


def _conf_kernel(val_ref, gate_ref, w_ref, b_ref, lng_ref, lnb_ref, y_ref, pad_ref):
    s = val_ref.shape[0]
    pad_ref[0:HIST, :] = jnp.zeros((HIST, LANES), F32)
    pad_ref[HIST:HIST + s, :] = val_ref[...].astype(F32) * jax.nn.sigmoid(gate_ref[...].astype(F32))

    def chunk(c, carry):
        base = pl.multiple_of(c * ROW_CHUNK, ROW_CHUNK)
        acc = jnp.zeros((ROW_CHUNK, LANES), F32)
        for k in range(CONF_K):
            shift = CONF_K - 1 - k
            acc = acc + pad_ref[pl.ds(base + HIST - shift, ROW_CHUNK), :] * w_ref[k:k + 1, :]
        u = acc + b_ref[...]
        mu = jnp.mean(u, axis=-1, keepdims=True)
        uc = u - mu
        var = jnp.mean(uc * uc, axis=-1, keepdims=True)
        ln = uc * lax.rsqrt(var + EPS) * lng_ref[...] + lnb_ref[...]
        y_ref[pl.ds(base, ROW_CHUNK), :] = (ln * jax.nn.sigmoid(ln)).astype(y_ref.dtype)
        return carry

    lax.fori_loop(0, s // ROW_CHUNK, chunk, 0)


def _conf_mixer(z3, conf_w, conf_b, ln_g, ln_b, col0, nheads):
    b, s, _ = z3.shape
    blk0 = col0 // LANES
    width = nheads * LANES

    def col(off):
        return pl.BlockSpec((None, s, LANES), lambda i, h: (i, 0, blk0 + off * nheads + h))

    vec = pl.BlockSpec((1, LANES), lambda i, h: (0, h))
    return pl.pallas_call(
        _conf_kernel,
        grid=(b, nheads),
        in_specs=[col(0), col(1), pl.BlockSpec((CONF_K, LANES), lambda i, h: (0, h)), vec, vec, vec],
        out_specs=pl.BlockSpec((None, s, LANES), lambda i, h: (i, 0, h)),
        out_shape=jax.ShapeDtypeStruct((b, s, width), BF16),
        scratch_shapes=[pltpu.VMEM((HIST + s, LANES), F32)],
        compiler_params=_params("parallel", "parallel"),
        name="conf_mixer",
    )(z3, z3, conf_w, conf_b.reshape(1, width), ln_g.reshape(1, width), ln_b.reshape(1, width))


def _outproj_kernel(ya_ref, yb_ref, yc_ref, w_ref, h_ref, g_ref, h1_ref, xn_ref):
    wa, wb = ya_ref.shape[1], yb_ref.shape[1]
    acc = jnp.dot(ya_ref[...], w_ref[0:wa, :], preferred_element_type=F32)
    acc = acc + jnp.dot(yb_ref[...], w_ref[wa:wa + wb, :], preferred_element_type=F32)
    acc = acc + jnp.dot(yc_ref[...], w_ref[wa + wb:, :], preferred_element_type=F32)
    h1 = h_ref[...] + acc
    h1_ref[...] = h1
    ms = jnp.mean(h1 * h1, axis=-1, keepdims=True)
    xn_ref[...] = _pack(h1 * lax.rsqrt(ms + EPS) * g_ref[...])


def _outproj(ya, yb, yc, w, h, g):
    t, d = h.shape
    tm = _tile(t, 512)

    def rows(width):
        return pl.BlockSpec((tm, width), lambda i: (i, 0))

    return pl.pallas_call(
        _outproj_kernel,
        grid=(t // tm,),
        in_specs=[rows(ya.shape[1]), rows(yb.shape[1]), rows(yc.shape[1]),
                  pl.BlockSpec(w.shape, lambda i: (0, 0)), rows(d), pl.BlockSpec((1, d), lambda i: (0, 0))],
        out_specs=[rows(d), pl.BlockSpec((tm // 2, d), lambda i: (i, 0))],
        out_shape=[jax.ShapeDtypeStruct((t, d), F32), jax.ShapeDtypeStruct((t // 2, d), jnp.uint32)],
        compiler_params=_params("parallel"),
        name="outproj_norm",
    )(ya, yb, yc, w, h, g.reshape(1, d))


def _first_max(vals, index, sentinel):
    m = jnp.max(vals, axis=0, keepdims=True)
    first = jnp.min(jnp.where(vals == m, index, sentinel), axis=0, keepdims=True)
    return index == first, m


def _top_ranks(scores, vals_refs):
    key = lax.broadcasted_iota(jnp.int32, scores[0].shape, 0).astype(F32)
    live = list(scores)
    rank = [jnp.full(sc.shape, float(PEER_TOPK), F32) for sc in scores]
    for r in range(PEER_TOPK):
        for c in range(len(live)):
            hit, m = _first_max(live[c], key, float(PEER_NKEYS))
            vals_refs[c][r:r + 1, :] = m
            live[c] = jnp.where(hit, -jnp.inf, live[c])
            rank[c] = jnp.where(hit, float(r), rank[c])
    return rank


def _candidate_index():
    nrow = (SUBLANES + 2) * SUBLANES
    row = lax.broadcasted_iota(jnp.int32, (nrow, LANES), 0)
    blk = lax.shift_right_logical(row, 3).astype(F32)
    sub = lax.bitwise_and(row, SUBLANES - 1).astype(F32)
    ci = jnp.where(blk <= 1, 0.0, jnp.where(blk <= SUBLANES, blk - 1, SUBLANES + sub))
    cj = jnp.where(blk == 1, SUBLANES + sub, jnp.where(blk <= SUBLANES, sub, 0.0))
    return ci * PEER_TOPK + cj, (ci + 1) * (cj + 1) <= PEER_TOPK


def _candidates(t1, t2, real):
    lo, hi = slice(0, SUBLANES), slice(SUBLANES, 2 * SUBLANES)
    blocks = [t1[0:1] + t2[lo], t1[0:1] + t2[hi]]
    blocks += [t1[i:i + 1] + t2[lo] for i in range(1, SUBLANES)]
    blocks += [t1[hi] + t2[0:1]]
    return jnp.where(real, jnp.concatenate(blocks, axis=0), -jnp.inf)


def _select_topk(cands, flat):
    live = list(cands)
    sel = [jnp.zeros(c.shape, F32) for c in cands]
    for _ in range(PEER_TOPK):
        for c in range(len(live)):
            hit, _m = _first_max(live[c], flat, float(PEER_TOPK * PEER_TOPK))
            live[c] = jnp.where(hit, -jnp.inf, live[c])
            sel[c] = jnp.where(hit, 1.0, sel[c])
    return sel


def _gate_planes(s1, s2, rank1, t1, t2, cand, sel, cnt_ref):
    hi = slice(SUBLANES, 2 * SUBLANES)
    cnt_ref[0:1, :] = jnp.sum(sel[0:2 * SUBLANES], axis=0, keepdims=True)
    for i in range(1, SUBLANES):
        cnt_ref[i:i + 1, :] = jnp.sum(sel[(i + 1) * SUBLANES:(i + 2) * SUBLANES], axis=0, keepdims=True)
    cnt_ref[hi, :] = sel[(SUBLANES + 1) * SUBLANES:]
    c1 = jnp.zeros(s1.shape, F32)
    for i in range(PEER_TOPK):
        c1 = jnp.where(rank1 == i, cnt_ref[i:i + 1, :], c1)
    top = t1[0:1] + t2[0:1]
    z = jnp.sum(jnp.where(sel > 0, jnp.exp(cand - top), 0.0), axis=0, keepdims=True)
    e1 = jnp.exp(s1 - t1[0:1])
    w2 = jnp.exp(s2 - t2[0:1]) / z
    return w2, c1, e1


def _route_kernel(xn_ref, wq_ref, k1_ref, k2_ref, r2_ref, w2_ref, c1_ref, e1_ref,
                  s1_ref, s2_ref, t_ref, cnt_ref):
    tr = 2 * xn_ref.shape[0]
    q = jnp.dot(_unpack(xn_ref[...]), wq_ref[...], preferred_element_type=F32)
    nt = (((1,), (1,)), ((), ()))
    for h in range(PEER_HEADS):
        qa = q[:, (2 * h) * PEER_NKEYS:(2 * h + 1) * PEER_NKEYS].astype(BF16)
        qb = q[:, (2 * h + 1) * PEER_NKEYS:(2 * h + 2) * PEER_NKEYS].astype(BF16)
        s1_ref[h] = lax.dot_general(k1_ref[h], qa, nt, preferred_element_type=F32)
        s2_ref[h] = lax.dot_general(k2_ref[h], qb, nt, preferred_element_type=F32)

    nsub = tr // LANES

    def body(h, carry):
        toks = [slice(i * LANES, (i + 1) * LANES) for i in range(nsub)]
        s1 = [s1_ref[h, :, tok] for tok in toks]
        s2 = [s2_ref[h, :, tok] for tok in toks]
        rank1, rank2 = [], []
        for i in range(nsub):
            ra, rb = _top_ranks([s1[i], s2[i]], [t_ref.at[i, 0], t_ref.at[i, 1]])
            rank1.append(ra)
            rank2.append(rb)
        t1 = [t_ref[i, 0] for i in range(nsub)]
        t2 = [t_ref[i, 1] for i in range(nsub)]
        flat, real = _candidate_index()
        cand = [_candidates(t1[i], t2[i], real) for i in range(nsub)]
        sel = _select_topk(cand, flat)
        for i in range(nsub):
            w2, c1, e1 = _gate_planes(s1[i], s2[i], rank1[i], t1[i], t2[i], cand[i], sel[i], cnt_ref.at[i])
            r2_ref[h, :, toks[i]] = rank2[i].astype(r2_ref.dtype)
            w2_ref[h, :, toks[i]] = w2.astype(w2_ref.dtype)
            c1_ref[h, :, toks[i]] = c1
            e1_ref[h, :, toks[i]] = e1
        return carry

    lax.fori_loop(0, PEER_HEADS, body, 0)


def _route(xn2, wq, k1, k2):
    t, d = 2 * xn2.shape[0], xn2.shape[1]
    tr = _tile(t, 256)
    plane = pl.BlockSpec((PEER_HEADS, PEER_NKEYS, tr), lambda i: (0, 0, i))
    dense = jax.ShapeDtypeStruct((PEER_HEADS, PEER_NKEYS, t), BF16)
    byrow = jax.ShapeDtypeStruct((PEER_HEADS, PEER_NKEYS, t), F32)
    keys = pl.BlockSpec(k1.shape, lambda i: (0, 0, 0))
    return pl.pallas_call(
        _route_kernel,
        grid=(t // tr,),
        in_specs=[pl.BlockSpec((tr // 2, d), lambda i: (i, 0)), pl.BlockSpec(wq.shape, lambda i: (0, 0)), keys, keys],
        out_specs=[plane] * 4,
        out_shape=[dense, dense, byrow, byrow],
        scratch_shapes=[
            pltpu.VMEM((PEER_HEADS, PEER_NKEYS, tr), F32),
            pltpu.VMEM((PEER_HEADS, PEER_NKEYS, tr), F32),
            pltpu.VMEM((tr // LANES, 2, PEER_TOPK, LANES), F32),
            pltpu.VMEM((tr // LANES, PEER_TOPK, LANES), F32),
        ],
        compiler_params=_params("parallel"),
        name="peer_route",
    )(xn2, wq, k1, k2)


def _expert_kernel(xn_ref, u_ref, v_ref, r2_ref, w2_ref, c1_ref, e1_ref, h1_ref, *rest, final_norm):
    if final_norm:
        g_ref, o_ref, pre_ref, p_ref = rest
    else:
        o_ref, pre_ref, p_ref = rest
    j = pl.program_id(1)
    tm = 2 * xn_ref.shape[0]
    half = EXPERT_SUB // 2
    nsub = u_ref.shape[0] // half
    nkey1 = EXPERT_SUB // PEER_NKEYS
    nblk = PEER_NKEYS // BF16_ROWS

    @pl.when(j == 0)
    def _():
        o_ref[...] = h1_ref[...]

    nt = (((1,), (1,)), ((), ()))
    tn = (((0,), (0,)), ((), ()))
    xn = _unpack(xn_ref[...])
    for s in range(nsub):
        pre_ref[s] = lax.dot_general(_unpack(u_ref[s * half:(s + 1) * half, :]), xn, nt,
                                     preferred_element_type=F32)
    for s in range(nsub):
        for a in range(nkey1):
            n1 = (j * nsub + s) * nkey1 + a
            gate = [None] * nblk
            for h in range(PEER_HEADS):
                c1 = jnp.broadcast_to(c1_ref[h, pl.ds(n1, 1), :], (BF16_ROWS, tm)).astype(BF16)
                e1 = jnp.broadcast_to(e1_ref[h, pl.ds(n1, 1), :], (BF16_ROWS, tm)).astype(BF16)
                for rb in range(nblk):
                    rows = slice(rb * BF16_ROWS, (rb + 1) * BF16_ROWS)
                    g = jnp.where(r2_ref[h, rows, :] < c1, w2_ref[h, rows, :], jnp.zeros((), BF16)) * e1
                    gate[rb] = g if gate[rb] is None else gate[rb] + g
            for rb in range(nblk):
                rows = slice(a * PEER_NKEYS + rb * BF16_ROWS, a * PEER_NKEYS + (rb + 1) * BF16_ROWS)
                pre = pre_ref[s, rows, :]
                act = 0.5 * pre * (1.0 + lax.erf(pre * (0.5 ** 0.5)))
                p_ref[s, rows, :] = act.astype(BF16) * gate[rb]
        o_ref[...] += lax.dot_general(p_ref[s], v_ref[s * EXPERT_SUB:(s + 1) * EXPERT_SUB, :], tn,
                                      preferred_element_type=F32)

    if final_norm:
        @pl.when(j == pl.num_programs(1) - 1)
        def _():
            y = o_ref[...]
            ms = jnp.mean(y * y, axis=-1, keepdims=True)
            o_ref[...] = y * lax.rsqrt(ms + EPS) * g_ref[...]


def _experts(xn2, u, v, r2, w2, c1, e1, h1, final_g):
    t, d = h1.shape
    ne = v.shape[0]
    tm = _tile(t, 512)
    ec = 2 * EXPERT_SUB
    rows = pl.BlockSpec((tm, d), lambda i, j: (i, 0))
    chunk = pl.BlockSpec((ec, d), lambda i, j: (j, 0))
    plane = pl.BlockSpec((PEER_HEADS, PEER_NKEYS, tm), lambda i, j: (0, 0, i))
    in_specs = [pl.BlockSpec((tm // 2, d), lambda i, j: (i, 0)), pl.BlockSpec((ec // 2, d), lambda i, j: (j, 0)),
                chunk, plane, plane, plane, plane,
                pl.BlockSpec((tm, d), lambda i, j: (i, 0), pipeline_mode=pl.Buffered(1))]
    args = [xn2, u, v, r2, w2, c1, e1, h1]
    if final_g is not None:
        in_specs.append(pl.BlockSpec((1, d), lambda i, j: (0, 0)))
        args.append(final_g.reshape(1, d))
    return pl.pallas_call(
        functools.partial(_expert_kernel, final_norm=final_g is not None),
        grid=(t // tm, ne // ec),
        in_specs=in_specs,
        out_specs=rows,
        out_shape=jax.ShapeDtypeStruct((t, d), F32),
        scratch_shapes=[pltpu.VMEM((ec // EXPERT_SUB, EXPERT_SUB, tm), F32),
                        pltpu.VMEM((ec // EXPERT_SUB, EXPERT_SUB, tm), BF16)],
        compiler_params=_params("parallel", "arbitrary"),
        name="peer_experts",
    )(*args)


def kernel(x, norm1_g, w_in, pool_w, pool_scale, sconv_w, conf_w, conf_b, conf_ln_g, conf_ln_b, w_out,
           norm2_g, peer_wq, peer_k1, peer_k2, peer_u, peer_v, final_g):
    b, s, d = x.shape
    depth = w_in.shape[0]
    pool_width = pool_scale.shape[1]
    sconv_width = sconv_w.shape[2]
    conf_width = conf_w.shape[2]
    assert pool_width == len(POOL_WINDOWS) * LANES and s % ROW_CHUNK == 0
    assert peer_k1.shape[1:] == (PEER_HEADS, PEER_NKEYS, PEER_NKEYS)
    h = x.reshape(b * s, d)
    for l in range(depth):
        last = l == depth - 1
        z = _rms_inproj(h, norm1_g[l], w_in[l].astype(BF16))
        z3 = z.reshape(b, s, z.shape[1])
        ya = _pool_mixer(z3, pool_w[l].astype(BF16), pool_scale[l])
        yb = _sconv_mixer(z3, sconv_w[l], pool_width, sconv_width // LANES)
        yc = _conf_mixer(z3, conf_w[l], conf_b[l], conf_ln_g[l], conf_ln_b[l],
                         pool_width + 3 * sconv_width, conf_width // LANES)
        h1, xn2 = _outproj(ya.reshape(b * s, -1), yb.reshape(b * s, -1), yc.reshape(b * s, -1),
                           w_out[l].astype(BF16), h, norm2_g[l])
        r2, w2, c1, e1 = _route(xn2, peer_wq[l].astype(BF16), peer_k1[l].astype(BF16), peer_k2[l].astype(BF16))
        h = _experts(xn2, _pack_rows(peer_u[l]), peer_v[l].astype(BF16), r2, w2, c1, e1, h1,
                     final_g if last else None)
    return h.reshape(b, s, d)
```
